```python
import math
import jax, jax.numpy as jnp
from jax import lax
import numpy as np

D_MODEL = 1024
BATCH = 16
SEQ = 4096
DEPTH = 2

N_EVEN = (DEPTH + 1) // 2
N_ODD = DEPTH // 2

POOL_WIDTH = D_MODEL // 2
POOL_WINDOWS = (2, 4, 8, 16)
POOL_GROUPS = len(POOL_WINDOWS)
POOL_GROUP_DIM = POOL_WIDTH // POOL_GROUPS
DIFF_WIDTH = D_MODEL // 2
DIFF_HEAD_DIM = 64
DIFF_HEADS = DIFF_WIDTH // (2 * DIFF_HEAD_DIM)
EVEN_INNER = POOL_WIDTH + DIFF_WIDTH
EVEN_IN_COLS = POOL_WIDTH + 3 * DIFF_WIDTH + EVEN_INNER
EVEN_SPLITS = (POOL_WIDTH, POOL_WIDTH + DIFF_WIDTH, POOL_WIDTH + 2 * DIFF_WIDTH, POOL_WIDTH + 3 * DIFF_WIDTH)
ATTN_BLOCK = 128

RET_HEADS = 8
RET_KEY_DIM = D_MODEL // RET_HEADS
RET_VAL_DIM = D_MODEL // RET_HEADS
RET_QK = RET_HEADS * RET_KEY_DIM
RET_INNER = RET_HEADS * RET_VAL_DIM
RET_IN_COLS = 2 * RET_QK + 2 * RET_INNER
RET_SPLITS = (RET_QK, 2 * RET_QK, 2 * RET_QK + RET_INNER)
RET_CHUNK = 128

EPS = 1e-6

kernel_name = "hybrid_pool_diffattn_retention_gated"


def rms_norm(x, gain):
    x32 = x.astype(jnp.float32)
    y = x32 * lax.rsqrt(jnp.mean(x32 * x32, axis=-1, keepdims=True) + EPS)
    return (y * gain.astype(jnp.float32)).astype(x.dtype)


def alibi_slopes(n):
    return jnp.exp2(-8.0 * jnp.arange(1, n + 1, dtype=jnp.float32) / n)


def multiscale_pool(u, pool_w, pool_scale):
    b, s, _ = u.shape
    ug = u.astype(jnp.float32).reshape(b, s, POOL_GROUPS, POOL_GROUP_DIM)
    c0 = jnp.pad(jnp.cumsum(ug, axis=1), ((0, 0), (1, 0), (0, 0), (0, 0)))
    t = jnp.arange(s)
    outs = []
    for g, w in enumerate(POOL_WINDOWS):
        cg = c0[:, :, g]
        lag = jnp.pad(cg[:, : s + 1 - w], ((0, 0), (w - 1, 0), (0, 0)))
        count = jnp.minimum(t + 1, w).astype(jnp.float32)[None, :, None]
        outs.append((cg[:, 1:] - lag) / count - ug[:, :, g])
    pooled = jnp.stack(outs, axis=2)
    mixed = jnp.einsum('bsgc,gcd->bsgd', pooled, pool_w.astype(jnp.float32))
    return mixed.reshape(b, s, POOL_WIDTH) * pool_scale.astype(jnp.float32)


def diff_attention(q, k, v, q_gain, k_gain, lam, lam_init, head_gain):
    b, s, _ = q.shape
    q = rms_norm(q.astype(jnp.float32).reshape(b, s, 2 * DIFF_HEADS, DIFF_HEAD_DIM), q_gain)
    k = rms_norm(k.astype(jnp.float32).reshape(b, s, 2 * DIFF_HEADS, DIFF_HEAD_DIM), k_gain)
    q = q.reshape(b, s, DIFF_HEADS, 2, DIFF_HEAD_DIM) * (DIFF_HEAD_DIM ** -0.5)
    k = k.reshape(b, s, DIFF_HEADS, 2, DIFF_HEAD_DIM)
    v = v.astype(jnp.float32).reshape(b, s, DIFF_HEADS, 2 * DIFF_HEAD_DIM)
    slopes = alibi_slopes(DIFF_HEADS)[:, None, None]
    pos = jnp.arange(s, dtype=jnp.float32)
    outs = []
    for blk in range(s // ATTN_BLOCK):
        q0 = blk * ATTN_BLOCK
        kend = q0 + ATTN_BLOCK
        qb = q[:, q0:kend]
        kb = k[:, :kend]
        vb = v[:, :kend]
        scores = jnp.einsum('bqhmd,bkhmd->bhmqk', qb, kb)
        dist = pos[q0:kend, None] - pos[None, :kend]
        bias = jnp.where(dist >= 0, -slopes * dist, -jnp.inf)
        p = jax.nn.softmax(scores + bias[None, :, None], axis=-1)
        a = p[:, :, 0] - lam * p[:, :, 1]
        outs.append(jnp.einsum('bhqk,bkhe->bqhe', a, vb))
    o = jnp.concatenate(outs, axis=1)
    o = rms_norm(o, head_gain) * (1.0 - lam_init)
    return o.reshape(b, s, DIFF_WIDTH)


def retention(q, k, v):
    b, s = q.shape[:2]
    n = s // RET_CHUNK
    log_g = jnp.log(1.0 - jnp.exp2(-5.0 - jnp.arange(RET_HEADS, dtype=jnp.float32)))
    j = jnp.arange(RET_CHUNK, dtype=jnp.float32)
    rel = j[:, None] - j[None, :]
    decay_mask = jnp.where(rel >= 0, jnp.exp(log_g[:, None, None] * jnp.maximum(rel, 0.0)), 0.0)
    xi = jnp.exp(log_g[:, None] * (j + 1.0))[None, :, :, None]
    zeta = jnp.exp(log_g[:, None] * (RET_CHUNK - 1.0 - j))[None, :, :, None]
    g_chunk = jnp.exp(log_g * RET_CHUNK)[None, :, None, None]

    def to_chunks(a, d):
        return a.astype(jnp.float32).reshape(b, n, RET_CHUNK, RET_HEADS, d).transpose(1, 0, 3, 2, 4)

    qc = to_chunks(q, RET_KEY_DIM)
    kc = to_chunks(k, RET_KEY_DIM) * (RET_KEY_DIM ** -0.5)
    vc = to_chunks(v, RET_VAL_DIM)

    def step(state, inp):
        qi, ki, vi = inp
        inner = jnp.einsum('bhqd,bhkd->bhqk', qi, ki) * decay_mask
        o = jnp.einsum('bhqk,bhkv->bhqv', inner, vi) + jnp.einsum('bhqd,bhdv->bhqv', qi, state) * xi
        state = state * g_chunk + jnp.einsum('bhkd,bhkv->bhdv', ki * zeta, vi)
        return state, o

    state0 = jnp.zeros((b, RET_HEADS, RET_KEY_DIM, RET_VAL_DIM), jnp.float32)
    _, o = lax.scan(step, state0, (qc, kc, vc))
    return o.transpose(1, 0, 3, 2, 4).reshape(b, s, RET_HEADS, RET_VAL_DIM)


def head_group_norm(o, gain, bias):
    b, s = o.shape[:2]
    mu = jnp.mean(o, axis=-1, keepdims=True)
    var = jnp.mean(jnp.square(o - mu), axis=-1, keepdims=True)
    y = ((o - mu) * lax.rsqrt(var + EPS)).reshape(b, s, RET_INNER)
    return y * gain.astype(jnp.float32) + bias.astype(jnp.float32)


def even_layer(x, ln_g, w_in, pool_w, pool_scale, q_gain, k_gain, lam_q1, lam_k1, lam_q2, lam_k2, head_gain, w_out, layer_idx):
    h = rms_norm(x, ln_g)
    proj = jnp.einsum('bsd,de->bse', h, w_in)
    u, q, k, v, z = jnp.split(proj, EVEN_SPLITS, axis=-1)
    a_out = multiscale_pool(u, pool_w, pool_scale)
    lam_init = 0.8 - 0.6 * math.exp(-0.3 * layer_idx)
    f32 = jnp.float32
    lam = (jnp.exp(jnp.sum(lam_q1.astype(f32) * lam_k1.astype(f32)))
           - jnp.exp(jnp.sum(lam_q2.astype(f32) * lam_k2.astype(f32))) + lam_init)
    b_out = diff_attention(q, k, v, q_gain, k_gain, lam, lam_init, head_gain)
    mixed = jnp.concatenate([a_out, b_out], axis=-1) * jax.nn.silu(z.astype(f32))
    return x + jnp.einsum('bse,ed->bsd', mixed.astype(x.dtype), w_out)


def odd_layer(x, ln_g, w_in, norm_g, norm_b, w_out):
    b, s, _ = x.shape
    h = rms_norm(x, ln_g)
    proj = jnp.einsum('bsd,de->bse', h, w_in)
    q, k, v, z = jnp.split(proj, RET_SPLITS, axis=-1)
    o = retention(q.reshape(b, s, RET_HEADS, RET_KEY_DIM),
                  k.reshape(b, s, RET_HEADS, RET_KEY_DIM),
                  v.reshape(b, s, RET_HEADS, RET_VAL_DIM))
    mixed = head_group_norm(o, norm_g, norm_b) * jax.nn.silu(z.astype(jnp.float32))
    return x + jnp.einsum('bse,ed->bsd', mixed.astype(x.dtype), w_out)


def setup_inputs(seed: int = 0) -> dict:
    key = jax.random.key(seed)
    ks = jax.random.split(key, 24)
    f32 = jnp.float32
    nrm = lambda k, shape, scale: jax.random.normal(k, shape, f32) * scale
    E, O = N_EVEN, N_ODD
    return {
        "x": jax.random.normal(ks[0], (BATCH, SEQ, D_MODEL), f32),
        "even_ln": 1.0 + nrm(ks[1], (E, D_MODEL), 0.05),
        "even_w_in": nrm(ks[2], (E, D_MODEL, EVEN_IN_COLS), D_MODEL ** -0.5),
        "even_pool_w": nrm(ks[3], (E, POOL_GROUPS, POOL_GROUP_DIM, POOL_GROUP_DIM), POOL_GROUP_DIM ** -0.5),
        "even_pool_scale": 1.0 + nrm(ks[4], (E, POOL_WIDTH), 0.1),
        "even_q_gain": 1.0 + nrm(ks[5], (E, DIFF_HEAD_DIM), 0.05),
        "even_k_gain": 1.0 + nrm(ks[6], (E, DIFF_HEAD_DIM), 0.05),
        "even_lam_q1": nrm(ks[7], (E, DIFF_HEAD_DIM), 0.1),
        "even_lam_k1": nrm(ks[8], (E, DIFF_HEAD_DIM), 0.1),
        "even_lam_q2": nrm(ks[9], (E, DIFF_HEAD_DIM), 0.1),
        "even_lam_k2": nrm(ks[10], (E, DIFF_HEAD_DIM), 0.1),
        "even_head_gain": 1.0 + nrm(ks[11], (E, 2 * DIFF_HEAD_DIM), 0.05),
        "even_w_out": nrm(ks[12], (E, EVEN_INNER, D_MODEL), EVEN_INNER ** -0.5),
        "odd_ln": 1.0 + nrm(ks[13], (O, D_MODEL), 0.05),
        "odd_w_in": nrm(ks[14], (O, D_MODEL, RET_IN_COLS), D_MODEL ** -0.5),
        "odd_norm_g": 1.0 + nrm(ks[15], (O, RET_INNER), 0.05),
        "odd_norm_b": nrm(ks[16], (O, RET_INNER), 0.02),
        "odd_w_out": nrm(ks[17], (O, RET_INNER, D_MODEL), RET_INNER ** -0.5),
    }


def reference(x, even_ln, even_w_in, even_pool_w, even_pool_scale, even_q_gain, even_k_gain,
              even_lam_q1, even_lam_k1, even_lam_q2, even_lam_k2, even_head_gain, even_w_out,
              odd_ln, odd_w_in, odd_norm_g, odd_norm_b, odd_w_out):
    for i in range(DEPTH):
        j = i // 2
        if i % 2 == 0:
            x = even_layer(x, even_ln[j], even_w_in[j], even_pool_w[j], even_pool_scale[j],
                           even_q_gain[j], even_k_gain[j], even_lam_q1[j], even_lam_k1[j],
                           even_lam_q2[j], even_lam_k2[j], even_head_gain[j], even_w_out[j], i)
        else:
            x = odd_layer(x, odd_ln[j], odd_w_in[j], odd_norm_g[j], odd_norm_b[j], odd_w_out[j])
    return x
```

```python
import functools
import math

import jax
import jax.numpy as jnp
from jax import lax
from jax.experimental import pallas as pl
from jax.experimental.pallas import tpu as pltpu

F32 = jnp.float32
BF16 = jnp.bfloat16

D_MODEL = 1024
EPS = 1e-6

POOL_WINDOWS = (2, 4, 8, 16)
POOL_GROUP_DIM = 128
POOL_WIDTH = 512
POOL_TAIL = 16
DIFF_WIDTH = 512
DIFF_HEAD_DIM = 64
DIFF_HEADS = 4
HEAD_W = 2 * DIFF_HEAD_DIM
EVEN_IN_COLS = 3072
LOG2E = math.log2(math.e)

RET_HEADS = 8
RET_DIM = 128
RET_CHUNK = 128
RET_IN_COLS = 4096

SEQ_TILE = 256
VMEM_LIMIT_BYTES = 56 * 1024 * 1024


def _dot(a, b):
    return jnp.dot(a, b, preferred_element_type=F32)


def _dot_nt(a, b):
    return lax.dot_general(a, b, (((1,), (1,)), ((), ())), preferred_element_type=F32)


def _rms_rows(x, gain):
    ms = jnp.mean(x * x, axis=-1, keepdims=True)
    return x * lax.rsqrt(ms + EPS) * gain


def _silu(z):
    return z * (1.0 / (1.0 + jnp.exp(-z)))


def _layer0_kernel(x_ref, ln_ref, win_ref, poolw_ref, pscale_ref, qg_ref, kg_ref, lam_ref,
                   hg_ref, wout_ref, segsum_ref, bias_ref, o_ref,
                   k_s, v_s, ubuf_s, qpad_s, acc_s, m_s, l_s, mixed_s, *, lam_init):
    T = SEQ_TILE
    i = pl.program_id(1)
    row0 = pl.multiple_of(i * T, T)

    x = x_ref[0]
    h = _rms_rows(x, ln_ref[...]).astype(BF16)

    u = _dot(h, win_ref[:, 0:POOL_WIDTH])

    @pl.when(i == 0)
    def _():
        ubuf_s[0:POOL_TAIL, :] = jnp.zeros((POOL_TAIL, POOL_WIDTH), F32)

    ubuf_s[POOL_TAIL:POOL_TAIL + T, :] = u
    t_glob = row0 + lax.broadcasted_iota(jnp.int32, (T, 1), 0)
    for g, w in enumerate(POOL_WINDOWS):
        lo, hi = g * POOL_GROUP_DIM, (g + 1) * POOL_GROUP_DIM
        s = ubuf_s[:, lo:hi]
        sh = 1
        while sh < w:
            s = s + pltpu.roll(s, sh, axis=0)
            sh *= 2
        cnt = jnp.minimum(t_glob + 1, w).astype(F32)
        pooled = s[POOL_TAIL:, :] * (1.0 / cnt) - u[:, lo:hi]
        mixed_g = _dot(pooled.astype(BF16), poolw_ref[g]) * pscale_ref[:, lo:hi]
        mixed_s[:, lo:hi] = mixed_g
    ubuf_s[0:POOL_TAIL, :] = u[T - POOL_TAIL:, :]

    def seg_norm(t, gain):
        sq = t * t
        hi_part = sq.astype(BF16)
        lo_part = (sq - hi_part.astype(F32)).astype(BF16)
        halves = []
        for c in range(0, DIFF_WIDTH, 256):
            ss = _dot(hi_part[:, c:c + 256], segsum_ref[...]) + _dot(lo_part[:, c:c + 256], segsum_ref[...])
            halves.append(ss)
        ss = jnp.concatenate(halves, axis=-1)
        return t * lax.rsqrt(ss * (1.0 / DIFF_HEAD_DIM) + EPS) * gain

    q = _dot(h, win_ref[:, 512:1024])
    qn = seg_norm(q, qg_ref[...]) * (DIFF_HEAD_DIM ** -0.5 * LOG2E)
    k = _dot(h, win_ref[:, 1024:1536])
    kn = seg_norm(k, kg_ref[...])
    v = _dot(h, win_ref[:, 1536:2048])
    k_s[pl.ds(row0, T), :] = kn.astype(BF16)
    v_s[pl.ds(row0, T), :] = v.astype(BF16)

    lane = lax.broadcasted_iota(jnp.int32, (T, HEAD_W), 1)
    first = lane < DIFF_HEAD_DIM
    for hd in range(DIFF_HEADS):
        qh = qn[:, hd * HEAD_W:(hd + 1) * HEAD_W]
        qpad_s[hd, 0:T, :] = jnp.where(first, qh, 0.0).astype(BF16)
        qpad_s[hd, T:2 * T, :] = jnp.where(first, 0.0, qh).astype(BF16)

    def scores(hd, kv_row0, bias_idx):
        kj = k_s[pl.ds(kv_row0, T), hd * HEAD_W:(hd + 1) * HEAD_W]
        s = _dot_nt(qpad_s[hd], kj)
        b = bias_ref[bias_idx]
        return jnp.concatenate([s[0:T] + b, s[T:2 * T] + b], axis=0)

    for hd in range(DIFF_HEADS):
        s = scores(hd, row0, DIFF_HEADS + hd)
        m = jnp.max(s, axis=-1, keepdims=True)
        p = jnp.exp2(s - m)
        vj = v_s[pl.ds(row0, T), hd * HEAD_W:(hd + 1) * HEAD_W]
        m_s[hd] = m
        l_s[hd] = jnp.sum(p, axis=-1, keepdims=True)
        acc_s[hd] = _dot(p.astype(BF16), vj)

    def kv_step(j, carry):
        kv_row0 = pl.multiple_of(j * T, T)
        dist = ((i - j) * T).astype(F32)
        for hd in range(DIFF_HEADS):
            slope2 = (2.0 ** (-8.0 * (hd + 1) / DIFF_HEADS)) * LOG2E
            s = scores(hd, kv_row0, hd)
            m_old = m_s[hd] + slope2 * dist
            m_new = jnp.maximum(m_old, jnp.max(s, axis=-1, keepdims=True))
            alpha = jnp.exp2(m_old - m_new)
            p = jnp.exp2(s - m_new)
            vj = v_s[pl.ds(kv_row0, T), hd * HEAD_W:(hd + 1) * HEAD_W]
            l_s[hd] = alpha * l_s[hd] + jnp.sum(p, axis=-1, keepdims=True)
            acc_s[hd] = alpha * acc_s[hd] + _dot(p.astype(BF16), vj)
            m_s[hd] = m_new - slope2 * dist
        return carry

    lax.fori_loop(0, i, kv_step, 0)

    lam_v = lam_ref[...]
    lam = (jnp.exp(jnp.sum(lam_v[0:1] * lam_v[1:2], axis=-1, keepdims=True))
           - jnp.exp(jnp.sum(lam_v[2:3] * lam_v[3:4], axis=-1, keepdims=True)) + lam_init)
    for hd in range(DIFF_HEADS):
        acc = acc_s[hd]
        inv_l = 1.0 / l_s[hd]
        o_h = acc[0:T] * inv_l[0:T] - lam * (acc[T:2 * T] * inv_l[T:2 * T])
        o_h = _rms_rows(o_h, hg_ref[...]) * (1.0 - lam_init)
        mixed_s[:, POOL_WIDTH + hd * HEAD_W:POOL_WIDTH + (hd + 1) * HEAD_W] = o_h

    z = _dot(h, win_ref[:, 2048:3072])
    gated = (mixed_s[...] * _silu(z)).astype(BF16)
    o_ref[0] = x + _dot(gated, wout_ref[...])


def _layer0(x, ln, w_in, pool_w, pool_scale, q_gain, k_gain, lam_vecs, head_gain, w_out, lam_init):
    B, S, D = x.shape
    T = SEQ_TILE
    nt = S // T

    seg = jnp.arange(256) // DIFF_HEAD_DIM
    segsum = (seg[:, None] == seg[None, :]).astype(BF16)
    r = jnp.arange(T, dtype=F32)
    rel = r[:, None] - r[None, :]
    slopes2 = jnp.exp2(-8.0 * jnp.arange(1, DIFF_HEADS + 1, dtype=F32) / DIFF_HEADS) * LOG2E
    bias_off = -slopes2[:, None, None] * rel[None]
    bias_diag = jnp.where(rel[None] >= 0, bias_off, -jnp.inf)
    bias = jnp.concatenate([bias_off, bias_diag], axis=0)

    const = lambda *shape: pl.BlockSpec(shape, lambda b, i: (0,) * len(shape))
    kernel = functools.partial(_layer0_kernel, lam_init=lam_init)
    return pl.pallas_call(
        kernel,
        grid=(B, nt),
        in_specs=[
            pl.BlockSpec((1, T, D), lambda b, i: (b, i, 0)),
            const(1, D),
            const(D, EVEN_IN_COLS),
            const(4, POOL_GROUP_DIM, POOL_GROUP_DIM),
            const(1, POOL_WIDTH),
            const(1, DIFF_WIDTH),
            const(1, DIFF_WIDTH),
            const(4, DIFF_HEAD_DIM),
            const(1, HEAD_W),
            const(D, D),
            const(256, 256),
            const(2 * DIFF_HEADS, T, T),
        ],
        out_specs=pl.BlockSpec((1, T, D), lambda b, i: (b, i, 0)),
        out_shape=jax.ShapeDtypeStruct((B, S, D), F32),
        scratch_shapes=[
            pltpu.VMEM((S, DIFF_WIDTH), BF16),
            pltpu.VMEM((S, DIFF_WIDTH), BF16),
            pltpu.VMEM((POOL_TAIL + T, POOL_WIDTH), F32),
            pltpu.VMEM((DIFF_HEADS, 2 * T, HEAD_W), BF16),
            pltpu.VMEM((DIFF_HEADS, 2 * T, HEAD_W), F32),
            pltpu.VMEM((DIFF_HEADS, 2 * T, 1), F32),
            pltpu.VMEM((DIFF_HEADS, 2 * T, 1), F32),
            pltpu.VMEM((T, D), F32),
        ],
        compiler_params=pltpu.CompilerParams(
            dimension_semantics=("arbitrary", "arbitrary"),
            vmem_limit_bytes=VMEM_LIMIT_BYTES),
        name="layer0_pool_diffattn",
    )(x, ln.reshape(1, D), w_in.astype(BF16), pool_w.astype(BF16), pool_scale.reshape(1, -1),
      jnp.tile(q_gain, 2 * DIFF_HEADS).reshape(1, -1), jnp.tile(k_gain, 2 * DIFF_HEADS).reshape(1, -1),
      lam_vecs, head_gain.reshape(1, -1), w_out.astype(BF16), segsum, bias)


def _layer1_kernel(x_ref, ln_ref, win_ref, dmask_ref, xi_ref, zeta_ref, gch_ref, ng_ref, nb_ref,
                   wout_ref, o_ref, state_s, mixed_s):
    T = SEQ_TILE
    C = RET_CHUNK
    i = pl.program_id(1)

    @pl.when(i == 0)
    def _():
        state_s[...] = jnp.zeros(state_s.shape, F32)

    x = x_ref[0]
    h = _rms_rows(x, ln_ref[...]).astype(BF16)
    q_all = _dot(h, win_ref[:, 0:1024])
    k_all = _dot(h, win_ref[:, 1024:2048])
    v_all = _dot(h, win_ref[:, 2048:3072])
    z_all = _dot(h, win_ref[:, 3072:4096])
    for hd in range(RET_HEADS):
        lo, hi = hd * RET_DIM, (hd + 1) * RET_DIM
        for c in range(T // C):
            rows = slice(c * C, (c + 1) * C)
            k_f = k_all[rows, lo:hi]
            qb = q_all[rows, lo:hi].astype(BF16)
            kb = k_f.astype(BF16)
            vb = v_all[rows, lo:hi].astype(BF16)
            state = state_s[hd]
            inner = _dot_nt(qb, kb) * dmask_ref[hd]
            o = _dot(inner.astype(BF16), vb) + _dot(qb, state.astype(BF16)) * xi_ref[hd]
            kz_t = (k_f * zeta_ref[hd]).T.astype(BF16)
            state_s[hd] = state * gch_ref[hd] + _dot(kz_t, vb)
            mu = jnp.mean(o, axis=-1, keepdims=True)
            d = o - mu
            var = jnp.mean(d * d, axis=-1, keepdims=True)
            y = d * lax.rsqrt(var + EPS) * ng_ref[:, lo:hi] + nb_ref[:, lo:hi]
            mixed_s[rows, lo:hi] = y * _silu(z_all[rows, lo:hi])
    o_ref[0] = x + _dot(mixed_s[...].astype(BF16), wout_ref[...])


def _layer1(x, ln, w_in, norm_g, norm_b, w_out):
    B, S, D = x.shape
    T = SEQ_TILE
    nt = S // T
    C = RET_CHUNK

    log_g = jnp.log(1.0 - jnp.exp2(-5.0 - jnp.arange(RET_HEADS, dtype=F32)))
    j = jnp.arange(C, dtype=F32)
    rel = j[:, None] - j[None, :]
    scale = RET_DIM ** -0.5
    dmask = jnp.where(rel >= 0, jnp.exp(log_g[:, None, None] * jnp.maximum(rel, 0.0)), 0.0) * scale
    ones = jnp.ones((1, 1, RET_DIM), F32)
    xi = jnp.exp(log_g[:, None] * (j + 1.0))[:, :, None] * ones
    zeta = jnp.exp(log_g[:, None] * (C - 1.0 - j))[:, :, None] * scale * ones
    gch = jnp.exp(log_g * C)[:, None, None] * jnp.ones((1, RET_DIM, RET_DIM), F32)

    const = lambda *shape: pl.BlockSpec(shape, lambda b, i: (0,) * len(shape))
    return pl.pallas_call(
        _layer1_kernel,
        grid=(B, nt),
        in_specs=[
            pl.BlockSpec((1, T, D), lambda b, i: (b, i, 0)),
            const(1, D),
            const(D, RET_IN_COLS),
            const(RET_HEADS, C, C),
            const(RET_HEADS, C, RET_DIM),
            const(RET_HEADS, C, RET_DIM),
            const(RET_HEADS, RET_DIM, RET_DIM),
            const(1, D),
            const(1, D),
            const(D, D),
        ],
        out_specs=pl.BlockSpec((1, T, D), lambda b, i: (b, i, 0)),
        out_shape=jax.ShapeDtypeStruct((B, S, D), F32),
        scratch_shapes=[
            pltpu.VMEM((RET_HEADS, RET_DIM, RET_DIM), F32),
            pltpu.VMEM((T, D), F32),
        ],
        compiler_params=pltpu.CompilerParams(
            dimension_semantics=("arbitrary", "arbitrary"),
            vmem_limit_bytes=VMEM_LIMIT_BYTES),
        name="layer1_retention",
    )(x, ln.reshape(1, D), w_in.astype(BF16), dmask, xi, zeta, gch,
      norm_g.reshape(1, D), norm_b.reshape(1, D), w_out.astype(BF16))


def kernel(x, even_ln, even_w_in, even_pool_w, even_pool_scale, even_q_gain, even_k_gain, even_lam_q1, even_lam_k1, even_lam_q2, even_lam_k2, even_head_gain, even_w_out, odd_ln, odd_w_in, odd_norm_g, odd_norm_b, odd_w_out):
    depth = even_ln.shape[0] + odd_ln.shape[0]
    for layer in range(depth):
        j = layer // 2
        if layer % 2 == 0:
            lam_init = 0.8 - 0.6 * math.exp(-0.3 * layer)
            lam_vecs = jnp.stack([even_lam_q1[j], even_lam_k1[j], even_lam_q2[j], even_lam_k2[j]]).astype(F32)
            x = _layer0(x, even_ln[j], even_w_in[j], even_pool_w[j], even_pool_scale[j], even_q_gain[j],
                        even_k_gain[j], lam_vecs, even_head_gain[j], even_w_out[j], lam_init)
        else:
            x = _layer1(x, odd_ln[j], odd_w_in[j], odd_norm_g[j], odd_norm_b[j], odd_w_out[j])
    return x
```

```python
import functools
import math

import jax
import jax.numpy as jnp
from jax import lax
from jax.experimental import pallas as pl
from jax.experimental.pallas import tpu as pltpu

F32 = jnp.float32
BF16 = jnp.bfloat16

D_MODEL = 1024
EPS = 1e-6

POOL_WINDOWS = (2, 4, 8, 16)
POOL_GROUP_DIM = 128
POOL_WIDTH = 512
POOL_TAIL = 16
DIFF_WIDTH = 512
DIFF_HEAD_DIM = 64
DIFF_HEADS = 4
HEAD_W = 2 * DIFF_HEAD_DIM
EVEN_IN_COLS = 3072
LOG2E = math.log2(math.e)

RET_HEADS = 8
RET_DIM = 128
RET_CHUNK = 128
RET_IN_COLS = 4096

SEQ_TILE = 256
VMEM_LIMIT_BYTES = 56 * 1024 * 1024


def _dot(a, b):
    return jnp.dot(a, b, preferred_element_type=F32)


def _dot_nt(a, b):
    return lax.dot_general(a, b, (((1,), (1,)), ((), ())), preferred_element_type=F32)


def _rms_rows(x, gain):
    ms = jnp.mean(x * x, axis=-1, keepdims=True)
    return x * lax.rsqrt(ms + EPS) * gain


def _silu(z):
    return z * (1.0 / (1.0 + jnp.exp(-z)))


def _layer0_kernel(x_ref, ln_ref, win_ref, poolw_ref, pscale_ref, qg_ref, kg_ref, lam_ref,
                   hg_ref, wout_ref, segsum_ref, bias_ref, o_ref,
                   k_s, vt_s, ubuf_s, qpad_s, acc_s, m_s, l_s, mixed_s, *, lam_init):
    T = SEQ_TILE
    i = pl.program_id(1)
    row0 = pl.multiple_of(i * T, T)

    x = x_ref[0]
    h = _rms_rows(x, ln_ref[...]).astype(BF16)

    u = _dot(h, win_ref[:, 0:POOL_WIDTH])

    @pl.when(i == 0)
    def _():
        ubuf_s[0:POOL_TAIL, :] = jnp.zeros((POOL_TAIL, POOL_WIDTH), F32)

    ubuf_s[POOL_TAIL:POOL_TAIL + T, :] = u
    t_glob = row0 + lax.broadcasted_iota(jnp.int32, (T, 1), 0)
    for g, w in enumerate(POOL_WINDOWS):
        lo, hi = g * POOL_GROUP_DIM, (g + 1) * POOL_GROUP_DIM
        s = ubuf_s[:, lo:hi]
        sh = 1
        while sh < w:
            s = s + pltpu.roll(s, sh, axis=0)
            sh *= 2
        cnt = jnp.minimum(t_glob + 1, w).astype(F32)
        pooled = s[POOL_TAIL:, :] * (1.0 / cnt) - u[:, lo:hi]
        mixed_g = _dot(pooled.astype(BF16), poolw_ref[g]) * pscale_ref[:, lo:hi]
        mixed_s[:, lo:hi] = mixed_g
    ubuf_s[0:POOL_TAIL, :] = u[T - POOL_TAIL:, :]

    def seg_norm(t, gain):
        sq = t * t
        hi_part = sq.astype(BF16)
        lo_part = (sq - hi_part.astype(F32)).astype(BF16)
        halves = []
        for c in range(0, DIFF_WIDTH, 256):
            ss = _dot(hi_part[:, c:c + 256], segsum_ref[...]) + _dot(lo_part[:, c:c + 256], segsum_ref[...])
            halves.append(ss)
        ss = jnp.concatenate(halves, axis=-1)
        return t * lax.rsqrt(ss * (1.0 / DIFF_HEAD_DIM) + EPS) * gain

    q = _dot(h, win_ref[:, 512:1024])
    qn = seg_norm(q, qg_ref[...]) * (DIFF_HEAD_DIM ** -0.5 * LOG2E)
    k = _dot(h, win_ref[:, 1024:1536])
    kn = seg_norm(k, kg_ref[...])
    v = _dot(h, win_ref[:, 1536:2048])
    k_s[i] = kn.astype(BF16)
    vt_s[i] = v.T.astype(BF16)

    qt = qn.T
    feat = lax.broadcasted_iota(jnp.int32, (HEAD_W, T), 0)
    first = feat < DIFF_HEAD_DIM
    for hd in range(DIFF_HEADS):
        qh = qt[hd * HEAD_W:(hd + 1) * HEAD_W, :]
        qpad_s[hd, :, 0:T] = jnp.where(first, qh, 0.0).astype(BF16)
        qpad_s[hd, :, T:2 * T] = jnp.where(first, 0.0, qh).astype(BF16)

    def scores_t(hd, j, bias_idx):
        kj = k_s[j, :, hd * HEAD_W:(hd + 1) * HEAD_W]
        s = _dot(kj, qpad_s[hd])
        b = bias_ref[bias_idx]
        return jnp.concatenate([s[:, 0:T] + b, s[:, T:2 * T] + b], axis=1)

    for hd in range(DIFF_HEADS):
        s = scores_t(hd, i, DIFF_HEADS + hd)
        m = jnp.max(s, axis=0, keepdims=True)
        p = jnp.exp2(s - m)
        m_s[hd] = m
        l_s[hd] = jnp.sum(p, axis=0, keepdims=True)
        acc_s[hd] = _dot(vt_s[i, hd * HEAD_W:(hd + 1) * HEAD_W, :], p.astype(BF16))

    def kv_step(j, carry):
        dist = ((i - j) * T).astype(F32)
        s_next = scores_t(0, j, 0)
        for hd in range(DIFF_HEADS):
            slope2 = (2.0 ** (-8.0 * (hd + 1) / DIFF_HEADS)) * LOG2E
            s = s_next
            if hd + 1 < DIFF_HEADS:
                s_next = scores_t(hd + 1, j, hd + 1)
            m_old = m_s[hd] + slope2 * dist
            m_new = jnp.maximum(m_old, jnp.max(s, axis=0, keepdims=True))
            alpha = jnp.exp2(m_old - m_new)
            p = jnp.exp2(s - m_new)
            l_s[hd] = alpha * l_s[hd] + jnp.sum(p, axis=0, keepdims=True)
            pv = _dot(vt_s[j, hd * HEAD_W:(hd + 1) * HEAD_W, :], p.astype(BF16))
            acc_s[hd] = alpha * acc_s[hd] + pv
            m_s[hd] = m_new - slope2 * dist
        return carry

    lax.fori_loop(0, i, kv_step, 0)

    lam_v = lam_ref[...]
    lam = (jnp.exp(jnp.sum(lam_v[0:1] * lam_v[1:2], axis=-1, keepdims=True))
           - jnp.exp(jnp.sum(lam_v[2:3] * lam_v[3:4], axis=-1, keepdims=True)) + lam_init)
    for hd in range(DIFF_HEADS):
        acc = acc_s[hd]
        inv_l = 1.0 / l_s[hd]
        o_t = acc[:, 0:T] * inv_l[:, 0:T] - lam * (acc[:, T:2 * T] * inv_l[:, T:2 * T])
        o_h = _rms_rows(o_t.T, hg_ref[...]) * (1.0 - lam_init)
        mixed_s[:, POOL_WIDTH + hd * HEAD_W:POOL_WIDTH + (hd + 1) * HEAD_W] = o_h

    z = _dot(h, win_ref[:, 2048:3072])
    gated = (mixed_s[...] * _silu(z)).astype(BF16)
    o_ref[0] = x + _dot(gated, wout_ref[...])


def _layer0(x, ln, w_in, pool_w, pool_scale, q_gain, k_gain, lam_vecs, head_gain, w_out, lam_init):
    B, S, D = x.shape
    T = SEQ_TILE
    nt = S // T

    seg = jnp.arange(256) // DIFF_HEAD_DIM
    segsum = (seg[:, None] == seg[None, :]).astype(BF16)
    r = jnp.arange(T, dtype=F32)
    rel = r[None, :] - r[:, None]
    slopes2 = jnp.exp2(-8.0 * jnp.arange(1, DIFF_HEADS + 1, dtype=F32) / DIFF_HEADS) * LOG2E
    bias_off = -slopes2[:, None, None] * rel[None]
    bias_diag = jnp.where(rel[None] >= 0, bias_off, -jnp.inf)
    bias = jnp.concatenate([bias_off, bias_diag], axis=0)

    const = lambda *shape: pl.BlockSpec(shape, lambda b, i: (0,) * len(shape))
    kernel = functools.partial(_layer0_kernel, lam_init=lam_init)
    return pl.pallas_call(
        kernel,
        grid=(B, nt),
        in_specs=[
            pl.BlockSpec((1, T, D), lambda b, i: (b, i, 0)),
            const(1, D),
            const(D, EVEN_IN_COLS),
            const(4, POOL_GROUP_DIM, POOL_GROUP_DIM),
            const(1, POOL_WIDTH),
            const(1, DIFF_WIDTH),
            const(1, DIFF_WIDTH),
            const(4, DIFF_HEAD_DIM),
            const(1, HEAD_W),
            const(D, D),
            const(256, 256),
            const(2 * DIFF_HEADS, T, T),
        ],
        out_specs=pl.BlockSpec((1, T, D), lambda b, i: (b, i, 0)),
        out_shape=jax.ShapeDtypeStruct((B, S, D), F32),
        scratch_shapes=[
            pltpu.VMEM((nt, T, DIFF_WIDTH), BF16),
            pltpu.VMEM((nt, DIFF_WIDTH, T), BF16),
            pltpu.VMEM((POOL_TAIL + T, POOL_WIDTH), F32),
            pltpu.VMEM((DIFF_HEADS, HEAD_W, 2 * T), BF16),
            pltpu.VMEM((DIFF_HEADS, HEAD_W, 2 * T), F32),
            pltpu.VMEM((DIFF_HEADS, 1, 2 * T), F32),
            pltpu.VMEM((DIFF_HEADS, 1, 2 * T), F32),
            pltpu.VMEM((T, D), F32),
        ],
        compiler_params=pltpu.CompilerParams(
            dimension_semantics=("arbitrary", "arbitrary"),
            vmem_limit_bytes=VMEM_LIMIT_BYTES),
        name="layer0_pool_diffattn",
    )(x, ln.reshape(1, D), w_in.astype(BF16), pool_w.astype(BF16), pool_scale.reshape(1, -1),
      jnp.tile(q_gain, 2 * DIFF_HEADS).reshape(1, -1), jnp.tile(k_gain, 2 * DIFF_HEADS).reshape(1, -1),
      lam_vecs, head_gain.reshape(1, -1), w_out.astype(BF16), segsum, bias)


def _layer1_kernel(x_ref, ln_ref, win_ref, dmask_ref, xi_ref, zeta_ref, gch_ref, ng_ref, nb_ref,
                   wout_ref, o_ref, state_s, mixed_s):
    T = SEQ_TILE
    C = RET_CHUNK
    i = pl.program_id(1)

    @pl.when(i == 0)
    def _():
        state_s[...] = jnp.zeros(state_s.shape, F32)

    x = x_ref[0]
    h = _rms_rows(x, ln_ref[...]).astype(BF16)
    q_all = _dot(h, win_ref[:, 0:1024])
    k_all = _dot(h, win_ref[:, 1024:2048])
    v_all = _dot(h, win_ref[:, 2048:3072])
    z_all = _dot(h, win_ref[:, 3072:4096])
    for hd in range(RET_HEADS):
        lo, hi = hd * RET_DIM, (hd + 1) * RET_DIM
        for c in range(T // C):
            rows = slice(c * C, (c + 1) * C)
            k_f = k_all[rows, lo:hi]
            qb = q_all[rows, lo:hi].astype(BF16)
            kb = k_f.astype(BF16)
            vb = v_all[rows, lo:hi].astype(BF16)
            state = state_s[hd]
            inner = _dot_nt(qb, kb) * dmask_ref[hd]
            o = _dot(inner.astype(BF16), vb) + _dot(qb, state.astype(BF16)) * xi_ref[hd]
            kz_t = (k_f * zeta_ref[hd]).T.astype(BF16)
            state_s[hd] = state * gch_ref[hd] + _dot(kz_t, vb)
            mu = jnp.mean(o, axis=-1, keepdims=True)
            d = o - mu
            var = jnp.mean(d * d, axis=-1, keepdims=True)
            y = d * lax.rsqrt(var + EPS) * ng_ref[:, lo:hi] + nb_ref[:, lo:hi]
            mixed_s[rows, lo:hi] = y * _silu(z_all[rows, lo:hi])
    o_ref[0] = x + _dot(mixed_s[...].astype(BF16), wout_ref[...])


def _layer1(x, ln, w_in, norm_g, norm_b, w_out):
    B, S, D = x.shape
    T = SEQ_TILE
    nt = S // T
    C = RET_CHUNK

    log_g = jnp.log(1.0 - jnp.exp2(-5.0 - jnp.arange(RET_HEADS, dtype=F32)))
    j = jnp.arange(C, dtype=F32)
    rel = j[:, None] - j[None, :]
    scale = RET_DIM ** -0.5
    dmask = jnp.where(rel >= 0, jnp.exp(log_g[:, None, None] * jnp.maximum(rel, 0.0)), 0.0) * scale
    ones = jnp.ones((1, 1, RET_DIM), F32)
    xi = jnp.exp(log_g[:, None] * (j + 1.0))[:, :, None] * ones
    zeta = jnp.exp(log_g[:, None] * (C - 1.0 - j))[:, :, None] * scale * ones
    gch = jnp.exp(log_g * C)[:, None, None] * jnp.ones((1, RET_DIM, RET_DIM), F32)

    const = lambda *shape: pl.BlockSpec(shape, lambda b, i: (0,) * len(shape))
    return pl.pallas_call(
        _layer1_kernel,
        grid=(B, nt),
        in_specs=[
            pl.BlockSpec((1, T, D), lambda b, i: (b, i, 0)),
            const(1, D),
            const(D, RET_IN_COLS),
            const(RET_HEADS, C, C),
            const(RET_HEADS, C, RET_DIM),
            const(RET_HEADS, C, RET_DIM),
            const(RET_HEADS, RET_DIM, RET_DIM),
            const(1, D),
            const(1, D),
            const(D, D),
        ],
        out_specs=pl.BlockSpec((1, T, D), lambda b, i: (b, i, 0)),
        out_shape=jax.ShapeDtypeStruct((B, S, D), F32),
        scratch_shapes=[
            pltpu.VMEM((RET_HEADS, RET_DIM, RET_DIM), F32),
            pltpu.VMEM((T, D), F32),
        ],
        compiler_params=pltpu.CompilerParams(
            dimension_semantics=("arbitrary", "arbitrary"),
            vmem_limit_bytes=VMEM_LIMIT_BYTES),
        name="layer1_retention",
    )(x, ln.reshape(1, D), w_in.astype(BF16), dmask, xi, zeta, gch,
      norm_g.reshape(1, D), norm_b.reshape(1, D), w_out.astype(BF16))


def kernel(x, even_ln, even_w_in, even_pool_w, even_pool_scale, even_q_gain, even_k_gain, even_lam_q1, even_lam_k1, even_lam_q2, even_lam_k2, even_head_gain, even_w_out, odd_ln, odd_w_in, odd_norm_g, odd_norm_b, odd_w_out):
    depth = even_ln.shape[0] + odd_ln.shape[0]
    for layer in range(depth):
        j = layer // 2
        if layer % 2 == 0:
            lam_init = 0.8 - 0.6 * math.exp(-0.3 * layer)
            lam_vecs = jnp.stack([even_lam_q1[j], even_lam_k1[j], even_lam_q2[j], even_lam_k2[j]]).astype(F32)
            x = _layer0(x, even_ln[j], even_w_in[j], even_pool_w[j], even_pool_scale[j], even_q_gain[j],
                        even_k_gain[j], lam_vecs, even_head_gain[j], even_w_out[j], lam_init)
        else:
            x = _layer1(x, odd_ln[j], odd_w_in[j], odd_norm_g[j], odd_norm_b[j], odd_w_out[j])
    return x
```

```python
import functools
import math

import jax
import jax.numpy as jnp
from jax import lax
from jax.experimental import pallas as pl
from jax.experimental.pallas import tpu as pltpu

F32 = jnp.float32
BF16 = jnp.bfloat16

D_MODEL = 1024
EPS = 1e-6
MXU_TILE = 256
BF16_ROWS = 16

POOL_WINDOWS = (2, 4, 8, 16)
POOL_GROUP_DIM = 128
POOL_WIDTH = 512
POOL_TAIL = 16
DIFF_WIDTH = 512
DIFF_HEAD_DIM = 64
DIFF_HEADS = 4
HEAD_W = 2 * DIFF_HEAD_DIM
EVEN_IN_COLS = 3072
LOG2E = math.log2(math.e)
V_ROWS = HEAD_W + BF16_ROWS

RET_HEADS = 8
RET_DIM = 128
RET_CHUNK = 128
RET_IN_COLS = 4096

SEQ_TILE = 512
KEY_TILE = 256
QK_AHEAD = 3
VMEM_LIMIT_BYTES = 58 * 1024 * 1024


def _dot(a, b):
    return jnp.dot(a, b, preferred_element_type=F32)


def _dot_nt(a, b):
    return lax.dot_general(a, b, (((1,), (1,)), ((), ())), preferred_element_type=F32)


def _rms_rows(x, gain):
    ms = jnp.mean(x * x, axis=-1, keepdims=True)
    return x * lax.rsqrt(ms + EPS) * gain


def _silu(z):
    return z * (1.0 / (1.0 + jnp.exp(-z)))


def _alibi_slope2(hd):
    return (2.0 ** (-8.0 * (hd + 1) / DIFF_HEADS)) * LOG2E


def _layer0_kernel(x_ref, ln_ref, win_ref, poolw_ref, pscale_ref, qg_ref, kg_ref, lam_ref,
                   hg_ref, wout_ref, segsum_ref, kw_ref, mask_ref, o_ref,
                   k_s, vt_s, ubuf_s, qpad_s, acc_s, m_s, mixed_s, *, lam_init):
    T = SEQ_TILE
    TK = KEY_TILE
    NQ = 2 * T
    i = pl.program_id(1)
    row0 = pl.multiple_of(i * T, T)

    x = x_ref[0]
    h = _rms_rows(x, ln_ref[...]).astype(BF16)

    u = _dot(h, win_ref[:, 0:POOL_WIDTH])

    @pl.when(i == 0)
    def _():
        ubuf_s[0:POOL_TAIL, :] = jnp.zeros((POOL_TAIL, POOL_WIDTH), F32)

    ubuf_s[POOL_TAIL:POOL_TAIL + T, :] = u
    t_glob = row0 + lax.broadcasted_iota(jnp.int32, (T, 1), 0)
    for g, w in enumerate(POOL_WINDOWS):
        lo, hi = g * POOL_GROUP_DIM, (g + 1) * POOL_GROUP_DIM
        s = ubuf_s[:, lo:hi]
        sh = 1
        while sh < w:
            s = s + pltpu.roll(s, sh, axis=0)
            sh *= 2
        cnt = jnp.minimum(t_glob + 1, w).astype(F32)
        pooled = s[POOL_TAIL:, :] * (1.0 / cnt) - u[:, lo:hi]
        mixed_g = _dot(pooled.astype(BF16), poolw_ref[g]) * pscale_ref[:, lo:hi]
        mixed_s[:, lo:hi] = mixed_g
    ubuf_s[0:POOL_TAIL, :] = u[T - POOL_TAIL:, :]

    def seg_norm(t, gain):
        sq = t * t
        hi_part = sq.astype(BF16)
        lo_part = (sq - hi_part.astype(F32)).astype(BF16)
        parts = []
        for c in range(0, DIFF_WIDTH, MXU_TILE):
            ss = (_dot(hi_part[:, c:c + MXU_TILE], segsum_ref[...])
                  + _dot(lo_part[:, c:c + MXU_TILE], segsum_ref[...]))
            parts.append(ss)
        ss = jnp.concatenate(parts, axis=-1)
        return t * lax.rsqrt(ss * (1.0 / DIFF_HEAD_DIM) + EPS) * gain

    q = _dot(h, win_ref[:, 512:1024])
    qn = seg_norm(q, qg_ref[...]) * (DIFF_HEAD_DIM ** -0.5 * LOG2E)
    k = _dot(h, win_ref[:, 1024:1536])
    kn = seg_norm(k, kg_ref[...]).astype(BF16)
    v = _dot(h, win_ref[:, 1536:2048])
    vt = v.T
    for t in range(T // TK):
        jt = 2 * i + t
        k_s[jt] = kn[t * TK:(t + 1) * TK, :]
        for hd in range(DIFF_HEADS):
            kw = kw_ref[hd]
            vt_s[jt, hd, 0:HEAD_W, :] = (vt[hd * HEAD_W:(hd + 1) * HEAD_W, t * TK:(t + 1) * TK] * kw).astype(BF16)
            vt_s[jt, hd, HEAD_W:V_ROWS, :] = jnp.broadcast_to(kw, (BF16_ROWS, TK)).astype(BF16)

    qt = qn.T
    feat = lax.broadcasted_iota(jnp.int32, (HEAD_W, T), 0)
    first = feat < DIFF_HEAD_DIM
    for hd in range(DIFF_HEADS):
        qh = qt[hd * HEAD_W:(hd + 1) * HEAD_W, :]
        qpad_s[hd, 0:HEAD_W, 0:T] = jnp.where(first, qh, 0.0).astype(BF16)
        qpad_s[hd, 0:HEAD_W, T:NQ] = jnp.where(first, 0.0, qh).astype(BF16)

    lane = lax.broadcasted_iota(jnp.int32, (1, NQ), 1)
    qq_row = jnp.where(lane >= T, lane - T, lane).astype(F32)

    upper = (slice(TK, T), slice(T + TK, NQ))

    def scores_t(jt, kind, hd):
        lhs = k_s[jt, :, hd * HEAD_W:(hd + 1) * HEAD_W]
        if kind == "half":
            w = jnp.concatenate([qpad_s[hd, :, sl] for sl in upper], axis=1)
        else:
            w = qpad_s[hd]
        return _dot(lhs, w)

    def run_tiles(tiles):
        units = [(jt, kind, base, hd) for (jt, kind, base) in tiles for hd in range(DIFF_HEADS)]
        pending = [scores_t(jt, kind, hd) for (jt, kind, _, hd) in units[:QK_AHEAD]]
        for n, (jt, kind, base, hd) in enumerate(units):
            half = kind == "half"
            s = pending.pop(0)
            if n + QK_AHEAD < len(units):
                nxt = units[n + QK_AHEAD]
                pending.append(scores_t(nxt[0], nxt[1], nxt[3]))
            if kind != "off":
                msk = mask_ref[...]
                nc = s.shape[1] // TK
                hit = range(nc) if half else range(0, nc, 2)
                s = jnp.concatenate(
                    [s[:, c * TK:(c + 1) * TK] + msk if c in hit else s[:, c * TK:(c + 1) * TK]
                     for c in range(nc)], axis=1)

            def rd(ref):
                return jnp.concatenate([ref[hd, :, sl] for sl in upper], axis=1) if half else ref[hd]

            def wr(ref, val):
                if half:
                    w = T - TK
                    for c, sl in enumerate(upper):
                        ref[hd, :, sl] = val[:, c * w:(c + 1) * w]
                else:
                    ref[hd] = val

            qq = jnp.concatenate([qq_row[:, sl] for sl in upper], axis=1) if half else qq_row
            c_row = _alibi_slope2(hd) * (base + qq - (TK - 1))
            smax = jnp.max(s, axis=0, keepdims=True)
            if kind == "first":
                m_new = smax
            else:
                m_old = rd(m_s) + c_row
                m_new = jnp.maximum(m_old, smax)
                alpha = jnp.exp2(m_old - m_new)
            p = jnp.exp2(s - m_new)
            pv = _dot(vt_s[jt, hd], p.astype(BF16))
            if kind == "first":
                wr(acc_s, pv)
            else:
                wr(acc_s, alpha * rd(acc_s) + pv)
            wr(m_s, m_new - c_row)

    run_tiles([(2 * i, "first", 0.0), (2 * i + 1, "half", float(-TK))])

    def kv_step(jj, carry):
        jt = 2 * jj
        base = ((i - jj) * T).astype(F32)
        run_tiles([(jt, "off", base), (jt + 1, "off", base - TK)])
        return carry

    lax.fori_loop(0, i, kv_step, 0)

    lam_v = lam_ref[...]
    lam = (jnp.exp(jnp.sum(lam_v[0:1] * lam_v[1:2], axis=-1, keepdims=True))
           - jnp.exp(jnp.sum(lam_v[2:3] * lam_v[3:4], axis=-1, keepdims=True)) + lam_init)
    for hd in range(DIFF_HEADS):
        acc = acc_s[hd]
        inv_l = 1.0 / acc[HEAD_W:HEAD_W + 1, :]
        o_t = acc[0:HEAD_W, 0:T] * inv_l[:, 0:T] - lam * (acc[0:HEAD_W, T:NQ] * inv_l[:, T:NQ])
        o_h = _rms_rows(o_t.T, hg_ref[...]) * (1.0 - lam_init)
        mixed_s[:, POOL_WIDTH + hd * HEAD_W:POOL_WIDTH + (hd + 1) * HEAD_W] = o_h

    z = _dot(h, win_ref[:, 2048:3072])
    gated = (mixed_s[...] * _silu(z)).astype(BF16)
    o_ref[0] = x + _dot(gated, wout_ref[...])


def _const_spec(*shape):
    return pl.BlockSpec(shape, lambda b, i: (0,) * len(shape), pipeline_mode=pl.Buffered(1))


def _layer0(x, ln, w_in, pool_w, pool_scale, q_gain, k_gain, lam_vecs, head_gain, w_out, lam_init):
    B, S, D = x.shape
    T = SEQ_TILE
    TK = KEY_TILE
    nt = S // T

    seg = jnp.arange(MXU_TILE) // DIFF_HEAD_DIM
    segsum = (seg[:, None] == seg[None, :]).astype(BF16)
    r = jnp.arange(TK, dtype=F32)
    mask = jnp.where(r[:, None] <= r[None, :], 0.0, -jnp.inf).astype(F32)
    slopes2 = jnp.exp2(-8.0 * jnp.arange(1, DIFF_HEADS + 1, dtype=F32) / DIFF_HEADS) * LOG2E
    kw = jnp.exp2(slopes2[:, None] * (r[None, :] - (TK - 1.0)))[:, None, :]

    kernel = functools.partial(_layer0_kernel, lam_init=lam_init)
    return pl.pallas_call(
        kernel,
        grid=(B, nt),
        in_specs=[
            pl.BlockSpec((1, T, D), lambda b, i: (b, i, 0)),
            _const_spec(1, D),
            _const_spec(D, EVEN_IN_COLS),
            _const_spec(4, POOL_GROUP_DIM, POOL_GROUP_DIM),
            _const_spec(1, POOL_WIDTH),
            _const_spec(1, DIFF_WIDTH),
            _const_spec(1, DIFF_WIDTH),
            _const_spec(4, DIFF_HEAD_DIM),
            _const_spec(1, HEAD_W),
            _const_spec(D, D),
            _const_spec(MXU_TILE, MXU_TILE),
            _const_spec(DIFF_HEADS, 1, TK),
            _const_spec(TK, TK),
        ],
        out_specs=pl.BlockSpec((1, T, D), lambda b, i: (b, i, 0)),
        out_shape=jax.ShapeDtypeStruct((B, S, D), F32),
        scratch_shapes=[
            pltpu.VMEM((S // TK, TK, DIFF_WIDTH), BF16),
            pltpu.VMEM((S // TK, DIFF_HEADS, V_ROWS, TK), BF16),
            pltpu.VMEM((POOL_TAIL + T, POOL_WIDTH), F32),
            pltpu.VMEM((DIFF_HEADS, HEAD_W, 2 * T), BF16),
            pltpu.VMEM((DIFF_HEADS, V_ROWS, 2 * T), F32),
            pltpu.VMEM((DIFF_HEADS, 1, 2 * T), F32),
            pltpu.VMEM((T, D), F32),
        ],
        compiler_params=pltpu.CompilerParams(
            dimension_semantics=("arbitrary", "arbitrary"),
            vmem_limit_bytes=VMEM_LIMIT_BYTES),
        name="layer0_pool_diffattn",
    )(x, ln.reshape(1, D), w_in.astype(BF16), pool_w.astype(BF16), pool_scale.reshape(1, -1),
      jnp.tile(q_gain, 2 * DIFF_HEADS).reshape(1, -1), jnp.tile(k_gain, 2 * DIFF_HEADS).reshape(1, -1),
      lam_vecs, head_gain.reshape(1, -1), w_out.astype(BF16), segsum, kw, mask)


def _layer1_kernel(x_ref, ln_ref, win_ref, dmask_ref, xi_ref, zeta_ref, gch_ref, ng_ref, nb_ref,
                   wout_ref, o_ref, state_s, mixed_s):
    T = SEQ_TILE
    C = RET_CHUNK
    i = pl.program_id(1)

    @pl.when(i == 0)
    def _():
        state_s[...] = jnp.zeros(state_s.shape, F32)

    x = x_ref[0]
    h = _rms_rows(x, ln_ref[...]).astype(BF16)
    q_all = _dot(h, win_ref[:, 0:1024])
    k_all = _dot(h, win_ref[:, 1024:2048])
    v_all = _dot(h, win_ref[:, 2048:3072])
    z_all = _dot(h, win_ref[:, 3072:4096])
    for c in range(T // C):
        rows = slice(c * C, (c + 1) * C)
        inner, cross, upd, vbs, states = [], [], [], [], []
        for hd in range(RET_HEADS):
            lo, hi = hd * RET_DIM, (hd + 1) * RET_DIM
            k_f = k_all[rows, lo:hi]
            qb = q_all[rows, lo:hi].astype(BF16)
            vb = v_all[rows, lo:hi].astype(BF16)
            state = state_s[hd]
            inner.append(_dot_nt(qb, k_f.astype(BF16)) * dmask_ref[hd])
            cross.append(_dot(qb, state.astype(BF16)))
            kz_t = (k_f * zeta_ref[hd]).T.astype(BF16)
            upd.append(_dot(kz_t, vb))
            vbs.append(vb)
            states.append(state)
        for hd in range(RET_HEADS):
            lo, hi = hd * RET_DIM, (hd + 1) * RET_DIM
            o = _dot(inner[hd].astype(BF16), vbs[hd]) + cross[hd] * xi_ref[hd]
            state_s[hd] = states[hd] * gch_ref[hd] + upd[hd]
            mu = jnp.mean(o, axis=-1, keepdims=True)
            d = o - mu
            var = jnp.mean(d * d, axis=-1, keepdims=True)
            y = d * lax.rsqrt(var + EPS) * ng_ref[:, lo:hi] + nb_ref[:, lo:hi]
            mixed_s[rows, lo:hi] = y * _silu(z_all[rows, lo:hi])
    o_ref[0] = x + _dot(mixed_s[...].astype(BF16), wout_ref[...])


def _layer1(x, ln, w_in, norm_g, norm_b, w_out):
    B, S, D = x.shape
    T = SEQ_TILE
    nt = S // T
    C = RET_CHUNK

    log_g = jnp.log(1.0 - jnp.exp2(-5.0 - jnp.arange(RET_HEADS, dtype=F32)))
    j = jnp.arange(C, dtype=F32)
    rel = j[:, None] - j[None, :]
    scale = RET_DIM ** -0.5
    dmask = jnp.where(rel >= 0, jnp.exp(log_g[:, None, None] * jnp.maximum(rel, 0.0)), 0.0) * scale
    ones = jnp.ones((1, 1, RET_DIM), F32)
    xi = jnp.exp(log_g[:, None] * (j + 1.0))[:, :, None] * ones
    zeta = jnp.exp(log_g[:, None] * (C - 1.0 - j))[:, :, None] * scale * ones
    gch = jnp.exp(log_g * C)[:, None, None] * jnp.ones((1, RET_DIM, RET_DIM), F32)

    return pl.pallas_call(
        _layer1_kernel,
        grid=(B, nt),
        in_specs=[
            pl.BlockSpec((1, T, D), lambda b, i: (b, i, 0)),
            _const_spec(1, D),
            _const_spec(D, RET_IN_COLS),
            _const_spec(RET_HEADS, C, C),
            _const_spec(RET_HEADS, C, RET_DIM),
            _const_spec(RET_HEADS, C, RET_DIM),
            _const_spec(RET_HEADS, RET_DIM, RET_DIM),
            _const_spec(1, D),
            _const_spec(1, D),
            _const_spec(D, D),
        ],
        out_specs=pl.BlockSpec((1, T, D), lambda b, i: (b, i, 0)),
        out_shape=jax.ShapeDtypeStruct((B, S, D), F32),
        scratch_shapes=[
            pltpu.VMEM((RET_HEADS, RET_DIM, RET_DIM), F32),
            pltpu.VMEM((T, D), F32),
        ],
        compiler_params=pltpu.CompilerParams(
            dimension_semantics=("arbitrary", "arbitrary"),
            vmem_limit_bytes=VMEM_LIMIT_BYTES),
        name="layer1_retention",
    )(x, ln.reshape(1, D), w_in.astype(BF16), dmask, xi, zeta, gch,
      norm_g.reshape(1, D), norm_b.reshape(1, D), w_out.astype(BF16))


def kernel(x, even_ln, even_w_in, even_pool_w, even_pool_scale, even_q_gain, even_k_gain, even_lam_q1, even_lam_k1, even_lam_q2, even_lam_k2, even_head_gain, even_w_out, odd_ln, odd_w_in, odd_norm_g, odd_norm_b, odd_w_out):
    depth = even_ln.shape[0] + odd_ln.shape[0]
    for layer in range(depth):
        j = layer // 2
        if layer % 2 == 0:
            lam_init = 0.8 - 0.6 * math.exp(-0.3 * layer)
            lam_vecs = jnp.stack([even_lam_q1[j], even_lam_k1[j], even_lam_q2[j], even_lam_k2[j]]).astype(F32)
            x = _layer0(x, even_ln[j], even_w_in[j], even_pool_w[j], even_pool_scale[j], even_q_gain[j],
                        even_k_gain[j], lam_vecs, even_head_gain[j], even_w_out[j], lam_init)
        else:
            x = _layer1(x, odd_ln[j], odd_w_in[j], odd_norm_g[j], odd_norm_b[j], odd_w_out[j])
    return x
```
